```python
import math
import jax, jax.numpy as jnp
from jax import lax
import numpy as np

D_MODEL = 1024
BATCH = 16
SEQ = 256
DEPTH = 4
DEC_BATCH = 8
DEC_SEQ = 4096
PAST_LEN = 512

GRID_W = 64
N_EVEN = (DEPTH + 1) // 2
N_ODD = DEPTH // 2
EPS = 1e-6
NEG = -1e30
CONV_W = 4
D_RNN = D_MODEL // 2
RG_BLOCKS = 8
RG_BLOCK = D_RNN // RG_BLOCKS
RG_C = 8.0
N_HEADS = 8
N_KV = 2
HEAD_DIM = 64
Q_GROUP = N_HEADS // N_KV
WINDOW = 128
ATT_BLOCK = 128
ROPE_AXIS = HEAD_DIM // 2
ROPE_BASE = 10000.0
EVEN_SPLIT = [D_RNN, 2 * D_RNN, 2 * D_RNN + N_HEADS * HEAD_DIM, 2 * D_RNN + (N_HEADS + N_KV) * HEAD_DIM]
EVEN_IN = 2 * D_RNN + (N_HEADS + 2 * N_KV) * HEAD_DIM
EVEN_OUT = D_RNN + N_HEADS * HEAD_DIM
DN_HEADS = 8
DN_DK = 128
DN_DV = 128
DN_CHUNK = 64
DN_HK = DN_HEADS * DN_DK
DN_HV = DN_HEADS * DN_DV
DN_QKV = 2 * DN_HK + DN_HV
ODD_IN = DN_QKV + DN_HV + 4 * DN_HEADS
D_FF = -(-8 * D_MODEL // (3 * 256)) * 256

kernel_name = "hybrid_flow_prefix_rglru_swa_gdn_step"


def rmsnorm(x, gain):
    xf = x.astype(jnp.float32)
    y = xf * lax.rsqrt(jnp.mean(xf * xf, axis=-1, keepdims=True) + EPS)
    return (y * gain.astype(jnp.float32)).astype(x.dtype)


def l2norm(x):
    return x * lax.rsqrt(jnp.sum(x * x, axis=-1, keepdims=True) + EPS)


def centred_conv(x, w, b=None):
    T = x.shape[1]
    left = CONV_W // 2
    xp = jnp.pad(x, ((0, 0), (left, CONV_W - 1 - left), (0, 0)))
    y = xp[:, 0:T] * w[0]
    for j in range(1, CONV_W):
        y = y + xp[:, j:j + T] * w[j]
    return y if b is None else y + b


def _lin_combine(left, right):
    a_l, b_l = left
    a_r, b_r = right
    return a_l * a_r, a_r * b_l + b_r


def rglru_scan(xc, w_a, b_a, w_x, b_x, lam, h0):
    B, T, _ = xc.shape
    xb = xc.reshape(B, T, RG_BLOCKS, RG_BLOCK)
    r = jax.nn.sigmoid(jnp.einsum('btnc,ncd->btnd', xb, w_a).reshape(B, T, D_RNN) + b_a)
    i = jax.nn.sigmoid(jnp.einsum('btnc,ncd->btnd', xb, w_x).reshape(B, T, D_RNN) + b_x)
    log_a = -RG_C * r * jax.nn.softplus(-lam.astype(jnp.float32))
    a = jnp.exp(log_a)
    b = jnp.sqrt(-jnp.expm1(2.0 * log_a)) * (i * xc)
    a_cum, b_cum = lax.associative_scan(_lin_combine, (a, b), axis=1)
    h = a_cum * h0[:, None, :] + b_cum
    return h, h[:, -1]


def axial_angles(n_tokens):
    rows = n_tokens // GRID_W
    row = jnp.repeat(jnp.arange(rows, dtype=jnp.float32), GRID_W)
    col = jnp.tile(jnp.arange(GRID_W, dtype=jnp.float32), rows)
    inv = ROPE_BASE ** (-jnp.arange(0, ROPE_AXIS, 2, dtype=jnp.float32) / ROPE_AXIS)
    return row[:, None] * inv, col[:, None] * inv


def rope_1d(x, ang):
    x1, x2 = jnp.split(x.astype(jnp.float32), 2, axis=-1)
    cos = jnp.cos(ang)[None, :, None, :]
    sin = jnp.sin(ang)[None, :, None, :]
    return jnp.concatenate([x1 * cos - x2 * sin, x1 * sin + x2 * cos], axis=-1)


def axial_rope(x, ang_r, ang_c):
    return jnp.concatenate([rope_1d(x[..., :ROPE_AXIS], ang_r), rope_1d(x[..., ROPE_AXIS:], ang_c)], axis=-1).astype(x.dtype)


def context_attention(q, k, v, sink):
    B, L = q.shape[:2]
    nb = L // ATT_BLOCK
    scale = HEAD_DIM ** -0.5
    qb = jnp.moveaxis(q.reshape(B, nb, ATT_BLOCK, N_KV, Q_GROUP, HEAD_DIM), 1, 0)
    sk = sink.astype(jnp.float32).reshape(N_KV, Q_GROUP)[None, :, :, None, None]

    def block(q_n):
        s = jnp.einsum('bikgd,bjkd->bkgij', q_n, k).astype(jnp.float32) * scale
        m = jnp.maximum(jnp.max(s, axis=-1, keepdims=True), sk)
        p = jnp.exp(s - m)
        den = jnp.sum(p, axis=-1, keepdims=True) + jnp.exp(sk - m)
        return jnp.einsum('bkgij,bjkd->bikgd', (p / den).astype(v.dtype), v)

    o = lax.map(block, qb)
    return jnp.moveaxis(o, 0, 1).reshape(B, L, N_HEADS * HEAD_DIM)


def latent_attention(q, k, v, ctx_k, ctx_v, sink):
    B, T = q.shape[:2]
    nb = T // ATT_BLOCK
    scale = HEAD_DIM ** -0.5
    qb = jnp.moveaxis(q.reshape(B, nb, ATT_BLOCK, N_KV, Q_GROUP, HEAD_DIM), 1, 0)
    pad = ((0, 0), (ATT_BLOCK, ATT_BLOCK), (0, 0), (0, 0))
    kp = jnp.pad(k, pad)
    vp = jnp.pad(v, pad)
    ctx_k = ctx_k.astype(q.dtype)
    ctx_v = ctx_v.astype(v.dtype)
    offs = jnp.arange(3 * ATT_BLOCK)[None, :] - ATT_BLOCK - jnp.arange(ATT_BLOCK)[:, None]
    in_window = jnp.abs(offs) <= WINDOW
    sk = sink.astype(jnp.float32).reshape(N_KV, Q_GROUP)[None, :, :, None, None]

    def block(args):
        n, q_n = args
        start = n * ATT_BLOCK
        k_n = lax.dynamic_slice_in_dim(kp, start, 3 * ATT_BLOCK, axis=1)
        v_n = lax.dynamic_slice_in_dim(vp, start, 3 * ATT_BLOCK, axis=1)
        kpos = start - ATT_BLOCK + jnp.arange(3 * ATT_BLOCK)
        valid = in_window & ((kpos >= 0) & (kpos < T))[None, :]
        s_w = jnp.einsum('bikgd,bjkd->bkgij', q_n, k_n).astype(jnp.float32) * scale
        s_w = jnp.where(valid, s_w, NEG)
        s_c = jnp.einsum('bikgd,bckd->bkgic', q_n, ctx_k).astype(jnp.float32) * scale
        m = jnp.maximum(jnp.maximum(jnp.max(s_w, axis=-1, keepdims=True), jnp.max(s_c, axis=-1, keepdims=True)), sk)
        p_w = jnp.exp(s_w - m)
        p_c = jnp.exp(s_c - m)
        den = jnp.sum(p_w, axis=-1, keepdims=True) + jnp.sum(p_c, axis=-1, keepdims=True) + jnp.exp(sk - m)
        return (jnp.einsum('bkgij,bjkd->bikgd', (p_w / den).astype(v.dtype), v_n)
                + jnp.einsum('bkgic,bckd->bikgd', (p_c / den).astype(v.dtype), ctx_v))

    o = lax.map(block, (jnp.arange(nb), qb))
    return jnp.moveaxis(o, 0, 1).reshape(B, T, N_HEADS * HEAD_DIM)


def chunk_gated_delta(q, k, v, g, beta, s0):
    B, T, H, _ = q.shape
    dv = v.shape[-1]
    n, C = T // DN_CHUNK, DN_CHUNK

    def chunks(t):
        return jnp.moveaxis(t.reshape((B, n, C, H) + t.shape[3:]), (1, 3), (0, 2))

    qc, kc, vc, bc = chunks(q), chunks(k), chunks(v), chunks(beta)
    gc = jnp.cumsum(chunks(g), axis=-1)
    idx = jnp.arange(C)
    incl = idx[:, None] >= idx[None, :]
    strict = idx[:, None] > idx[None, :]
    decay = jnp.exp(jnp.where(incl, gc[..., :, None] - gc[..., None, :], NEG))
    kk = jnp.einsum('nbhid,nbhjd->nbhij', kc, kc)
    a_mat = jnp.where(strict, bc[..., :, None] * kk * decay, 0.0) + jnp.eye(C, dtype=jnp.float32)
    rhs = jnp.concatenate([vc * bc[..., None], kc * (bc * jnp.exp(gc))[..., None]], axis=-1)
    sol = lax.linalg.triangular_solve(a_mat, rhs, left_side=True, lower=True, unit_diagonal=True)
    u, w = sol[..., :dv], sol[..., dv:]
    qk = jnp.einsum('nbhid,nbhjd->nbhij', qc, kc) * decay
    qg = qc * jnp.exp(gc)[..., None]
    kd = kc * jnp.exp(gc[..., -1:] - gc)[..., None]
    g_last = jnp.exp(gc[..., -1])

    def step(S, xs):
        qg_c, kd_c, u_c, w_c, qk_c, gl_c = xs
        v_new = u_c - jnp.einsum('bhck,bhkv->bhcv', w_c, S)
        o = jnp.einsum('bhck,bhkv->bhcv', qg_c, S) + jnp.einsum('bhij,bhjv->bhiv', qk_c, v_new)
        S = S * gl_c[..., None, None] + jnp.einsum('bhck,bhcv->bhkv', kd_c, v_new)
        return S, o

    s_final, o = lax.scan(step, s0, (qg, kd, u, w, qk, g_last))
    o = jnp.moveaxis(o, (0, 2), (1, 3)).reshape(B, T, H, dv)
    return o, s_final


def even_mixer(h, w_in, conv_w, conv_b, rg_wa, rg_ba, rg_wx, rg_bx, rg_lam, sink, w_out, ctx_k, ctx_v, h0):
    B, T, _ = h.shape
    xr, gr, q, k, v = jnp.split(h @ w_in, EVEN_SPLIT, axis=-1)
    xc = centred_conv(xr, conv_w, conv_b).astype(jnp.float32)
    hf, sf = rglru_scan(xc, rg_wa[0], rg_ba[0], rg_wx[0], rg_bx[0], rg_lam[0], h0[:, 0])
    hb, sb = rglru_scan(xc[:, ::-1], rg_wa[1], rg_ba[1], rg_wx[1], rg_bx[1], rg_lam[1], h0[:, 1])
    y_r = (hf + hb[:, ::-1]).astype(h.dtype) * jax.nn.gelu(gr)
    q = q.reshape(B, T, N_HEADS, HEAD_DIM)
    k = k.reshape(B, T, N_KV, HEAD_DIM)
    v = v.reshape(B, T, N_KV, HEAD_DIM)
    if ctx_k is None:
        y_a = context_attention(q, k, v, sink)
    else:
        ang_r, ang_c = axial_angles(T)
        q = axial_rope(q, ang_r, ang_c)
        k = axial_rope(k, ang_r, ang_c)
        y_a = latent_attention(q, k, v, ctx_k, ctx_v, sink)
    out = jnp.concatenate([y_r, y_a], axis=-1) @ w_out
    return out, jnp.stack([sf, sb], axis=1), k, v


def odd_mixer(h, w_in, conv_w, a_log, dt_bias, o_norm, w_out, s0):
    B, T, _ = h.shape
    qkv, z, ab = jnp.split(h @ w_in, [DN_QKV, DN_QKV + DN_HV], axis=-1)
    qkv = jax.nn.silu(centred_conv(qkv, conv_w)).astype(jnp.float32)
    q, k, v = jnp.split(qkv, [DN_HK, 2 * DN_HK], axis=-1)
    q = l2norm(q.reshape(B, T, DN_HEADS, DN_DK)) * (DN_DK ** -0.5)
    k = l2norm(k.reshape(B, T, DN_HEADS, DN_DK))
    v = v.reshape(B, T, DN_HEADS, DN_DV)
    ab = ab.astype(jnp.float32).reshape(B, T, 2, 2, DN_HEADS)
    g = -jnp.exp(a_log.astype(jnp.float32)) * jax.nn.softplus(ab[:, :, 0] + dt_bias.astype(jnp.float32))
    beta = jax.nn.sigmoid(ab[:, :, 1])
    o_f, s_f = chunk_gated_delta(q, k, v, g[:, :, 0], beta[:, :, 0], s0[:, 0])
    o_b, s_b = chunk_gated_delta(q[:, ::-1], k[:, ::-1], v[:, ::-1], g[:, ::-1, 1], beta[:, ::-1, 1], s0[:, 1])
    o = rmsnorm(o_f + o_b[:, ::-1], o_norm) * jax.nn.silu(z.astype(jnp.float32).reshape(B, T, DN_HEADS, DN_DV))
    out = o.reshape(B, T, DN_HV).astype(h.dtype) @ w_out
    return out, jnp.stack([s_f, s_b], axis=1)


def swiglu(h, w_gu, w_down):
    gate, up = jnp.split(h @ w_gu, 2, axis=-1)
    return (jax.nn.silu(gate) * up) @ w_down


def run_trunk(x, cond, weights, caches):
    (w_mod, b_mod, g_mix, g_ffn, g_final, w_in_even, conv_rg, conv_rg_b, rg_wa, rg_ba, rg_wx, rg_bx,
     rg_lam, attn_sink, w_out_even, w_in_odd, conv_dn, dn_a_log, dn_dt_bias, dn_o_norm, w_out_odd,
     w_gu, w_down) = weights
    B = x.shape[0]
    ks, vs, rgs, dns = [], [], [], []
    for l in range(DEPTH):
        i = l // 2
        sh1, sc1, ga1, sh2, sc2, ga2 = jnp.split(jax.nn.silu(cond) @ w_mod[l] + b_mod[l], 6, axis=-1)
        h = rmsnorm(x, g_mix[l]) * (1 + sc1) + sh1
        if l % 2 == 0:
            if caches is None:
                ck, cv = None, None
                h0 = jnp.zeros((B, 2, D_RNN), jnp.float32)
            else:
                ck, cv, h0 = caches[0][:, i], caches[1][:, i], caches[2][:, i].astype(jnp.float32)
            out, st, k, v = even_mixer(h, w_in_even[i], conv_rg[i], conv_rg_b[i], rg_wa[i], rg_ba[i], rg_wx[i],
                                       rg_bx[i], rg_lam[i], attn_sink[i], w_out_even[i], ck, cv, h0)
            ks.append(k)
            vs.append(v)
            rgs.append(st)
        else:
            if caches is None:
                s0 = jnp.zeros((B, 2, DN_HEADS, DN_DK, DN_DV), jnp.float32)
            else:
                s0 = caches[3][:, i].astype(jnp.float32)
            out, st = odd_mixer(h, w_in_odd[i], conv_dn[i], dn_a_log[i], dn_dt_bias[i], dn_o_norm[i], w_out_odd[i], s0)
            dns.append(st)
        x = x + ga1 * out
        h = rmsnorm(x, g_ffn[l]) * (1 + sc2) + sh2
        x = x + ga2 * swiglu(h, w_gu[l], w_down[l])
    y = rmsnorm(x, g_final)
    return y, jnp.stack(ks, axis=1), jnp.stack(vs, axis=1), jnp.stack(rgs, axis=1), jnp.stack(dns, axis=1)


def setup_inputs(seed: int = 0) -> dict:
    key = jax.random.key(seed)
    keys = iter(jax.random.split(key, 40))

    def nrm(shape, scale):
        return jax.random.normal(next(keys), shape, jnp.float32) * scale

    def unif(shape, lo, hi):
        return jax.random.uniform(next(keys), shape, jnp.float32, lo, hi)

    x_prompt = nrm((BATCH, SEQ, D_MODEL), 1.0)
    x_sample = nrm((DEC_BATCH, DEC_SEQ, D_MODEL), 1.0)
    c = nrm((DEC_BATCH, D_MODEL), 1.0)
    cache_attn_k = nrm((DEC_BATCH, N_EVEN, PAST_LEN, N_KV, HEAD_DIM), 1.0)
    cache_attn_v = nrm((DEC_BATCH, N_EVEN, PAST_LEN, N_KV, HEAD_DIM), 1.0)
    state_rglru = nrm((DEC_BATCH, N_EVEN, 2, D_RNN), 0.5)
    state_delta = nrm((DEC_BATCH, N_ODD, 2, DN_HEADS, DN_DK, DN_DV), 0.1)
    c_ctx = nrm((D_MODEL,), 1.0)
    w_mod = nrm((DEPTH, D_MODEL, 6 * D_MODEL), D_MODEL ** -0.5)
    b_mod = nrm((DEPTH, 6 * D_MODEL), 0.02)
    g_mix = 1.0 + nrm((DEPTH, D_MODEL), 0.05)
    g_ffn = 1.0 + nrm((DEPTH, D_MODEL), 0.05)
    g_final = 1.0 + nrm((D_MODEL,), 0.05)
    w_in_even = nrm((N_EVEN, D_MODEL, EVEN_IN), D_MODEL ** -0.5)
    conv_rg = nrm((N_EVEN, CONV_W, D_RNN), CONV_W ** -0.5)
    conv_rg_b = nrm((N_EVEN, D_RNN), 0.02)
    rg_wa = nrm((N_EVEN, 2, RG_BLOCKS, RG_BLOCK, RG_BLOCK), RG_BLOCK ** -0.5)
    rg_ba = nrm((N_EVEN, 2, D_RNN), 0.02)
    rg_wx = nrm((N_EVEN, 2, RG_BLOCKS, RG_BLOCK, RG_BLOCK), RG_BLOCK ** -0.5)
    rg_bx = nrm((N_EVEN, 2, D_RNN), 0.02)
    a_c = unif((N_EVEN, 2, D_RNN), 0.9, 0.999) ** (1.0 / RG_C)
    rg_lam = jnp.log(a_c) - jnp.log1p(-a_c)
    attn_sink = nrm((N_EVEN, N_HEADS), 0.5)
    w_out_even = nrm((N_EVEN, EVEN_OUT, D_MODEL), EVEN_OUT ** -0.5)
    w_in_odd = nrm((N_ODD, D_MODEL, ODD_IN), D_MODEL ** -0.5)
    conv_dn = nrm((N_ODD, CONV_W, DN_QKV), CONV_W ** -0.5)
    dn_a_log = jnp.log(unif((N_ODD, 2, DN_HEADS), 1.0, 16.0))
    dt = jnp.exp(unif((N_ODD, 2, DN_HEADS), math.log(1e-3), math.log(1e-1)))
    dn_dt_bias = dt + jnp.log(-jnp.expm1(-dt))
    dn_o_norm = 1.0 + nrm((N_ODD, DN_DV), 0.05)
    w_out_odd = nrm((N_ODD, DN_HV, D_MODEL), DN_HV ** -0.5)
    w_gu = nrm((DEPTH, D_MODEL, 2 * D_FF), D_MODEL ** -0.5)
    w_down = nrm((DEPTH, D_FF, D_MODEL), D_FF ** -0.5)
    return {"x_prompt": x_prompt, "x_sample": x_sample, "c": c,
            "cache_attn_k": cache_attn_k, "cache_attn_v": cache_attn_v,
            "state_rglru": state_rglru, "state_delta": state_delta, "c_ctx": c_ctx,
            "w_mod": w_mod, "b_mod": b_mod, "g_mix": g_mix, "g_ffn": g_ffn, "g_final": g_final,
            "w_in_even": w_in_even, "conv_rg": conv_rg, "conv_rg_b": conv_rg_b, "rg_wa": rg_wa,
            "rg_ba": rg_ba, "rg_wx": rg_wx, "rg_bx": rg_bx, "rg_lam": rg_lam, "attn_sink": attn_sink,
            "w_out_even": w_out_even, "w_in_odd": w_in_odd, "conv_dn": conv_dn, "dn_a_log": dn_a_log,
            "dn_dt_bias": dn_dt_bias, "dn_o_norm": dn_o_norm, "w_out_odd": w_out_odd,
            "w_gu": w_gu, "w_down": w_down}


def reference(x_prompt, x_sample, c, cache_attn_k, cache_attn_v, state_rglru, state_delta, c_ctx,
              w_mod, b_mod, g_mix, g_ffn, g_final, w_in_even, conv_rg, conv_rg_b, rg_wa, rg_ba, rg_wx,
              rg_bx, rg_lam, attn_sink, w_out_even, w_in_odd, conv_dn, dn_a_log, dn_dt_bias, dn_o_norm,
              w_out_odd, w_gu, w_down):
    weights = (w_mod, b_mod, g_mix, g_ffn, g_final, w_in_even, conv_rg, conv_rg_b, rg_wa, rg_ba, rg_wx,
               rg_bx, rg_lam, attn_sink, w_out_even, w_in_odd, conv_dn, dn_a_log, dn_dt_bias, dn_o_norm,
               w_out_odd, w_gu, w_down)
    y_prompt, new_k, new_v, new_rg, new_dn = run_trunk(x_prompt, c_ctx[None, None, :], weights, None)
    y_sample = run_trunk(x_sample, c[:, None, :], weights,
                         (cache_attn_k, cache_attn_v, state_rglru, state_delta))[0]
    return (y_prompt, y_sample, new_k, new_v, new_rg, new_dn)
```

```python
import functools
import math

import jax
import jax.numpy as jnp
from jax import lax
from jax.experimental import pallas as pl
from jax.experimental.pallas import tpu as pltpu

F32 = jnp.float32
BF16 = jnp.bfloat16

D_MODEL = 1024
DEPTH = 4
GRID_W = 64
EPS = 1e-6
NEG = -1e30
CONV_W = 4
D_RNN = 512
RG_BLOCKS = 8
RG_BLOCK = 64
RG_C = 8.0
N_HEADS = 8
N_KV = 2
HEAD_DIM = 64
Q_GROUP = N_HEADS // N_KV
WINDOW = 128
ROPE_AXIS = HEAD_DIM // 2
ROPE_BASE = 10000.0
DN_HEADS = 8
DN_DK = 128
DN_DV = 128
DN_CHUNK = 64
DN_QKV = 3 * DN_HEADS * DN_DK
D_FF = 2816

LANES = 128
SUBLANES = 8
VMEM_BYTES_V7X = 64 * 1024 * 1024
VMEM_LIMIT = VMEM_BYTES_V7X * 7 // 8

ROW_TILE = 512
ATTN_Q_TILE = 256

N_MOD_ROWS = 16


def _cparams(sem):
    return pltpu.CompilerParams(dimension_semantics=sem, vmem_limit_bytes=VMEM_LIMIT)


def _resident():
    return pl.BlockSpec(memory_space=pltpu.VMEM)


def _row_tile(t, cap=ROW_TILE):
    tm = min(t, cap)
    assert t % tm == 0, (t, tm)
    return tm


def _silu(x):
    return x * jax.nn.sigmoid(x)


def _norm_mod(x, gain, sc, sh):
    ms = jnp.mean(x * x, axis=-1, keepdims=True)
    return (x * lax.rsqrt(ms + EPS) * gain) * (1.0 + sc) + sh


def _mod_kernel(c_ref, w_ref, b_ref, o_ref):
    c = c_ref[...]
    s = _silu(c).astype(BF16)
    o_ref[...] = jnp.dot(s, w_ref[...].astype(BF16), preferred_element_type=F32) + b_ref[...]


def _modulation(cond, w_mod, b_mod):
    b3 = b_mod.reshape(DEPTH, 1, 6 * D_MODEL)
    out = pl.pallas_call(
        _mod_kernel,
        grid=(DEPTH, 6),
        in_specs=[
            pl.BlockSpec((N_MOD_ROWS, D_MODEL), lambda l, j: (0, 0)),
            pl.BlockSpec((None, D_MODEL, D_MODEL), lambda l, j: (l, 0, j)),
            pl.BlockSpec((None, 1, D_MODEL), lambda l, j: (l, 0, j)),
        ],
        out_specs=pl.BlockSpec((None, None, N_MOD_ROWS, D_MODEL), lambda l, j: (l, j, 0, 0)),
        out_shape=jax.ShapeDtypeStruct((DEPTH, 6, N_MOD_ROWS, D_MODEL), F32),
        compiler_params=_cparams(("arbitrary", "arbitrary")),
        name="adaln_mod",
    )(cond, w_mod, b3)
    return out.reshape(DEPTH, 6, N_MOD_ROWS, 1, D_MODEL)


def _mod_spec(layer, kind, ctx):
    if ctx:
        return pl.BlockSpec((None, None, None, 1, D_MODEL), lambda b, i: (layer, kind, N_MOD_ROWS // 2, 0, 0))
    return pl.BlockSpec((None, None, None, 1, D_MODEL), lambda b, i: (layer, kind, b, 0, 0))


def _rope_cols(r, cos, sin):
    lane = lax.broadcasted_iota(jnp.int32, (1, LANES), 1)
    first = (lane % ROPE_AXIS) < (ROPE_AXIS // 2)
    outs = []
    for j in range(r.shape[1] // LANES):
        xc = r[:, j * LANES:(j + 1) * LANES]
        fwd = pltpu.roll(xc, LANES - ROPE_AXIS // 2, 1)
        bwd = pltpu.roll(xc, ROPE_AXIS // 2, 1)
        outs.append(xc * cos + jnp.where(first, fwd, bwd) * sin)
    return outs[0] if len(outs) == 1 else jnp.concatenate(outs, axis=1)


def _inproj_kernel(x_ref, g_ref, sc_ref, sh_ref, w_ref, *rest, seg, rope):
    if any(rope):
        cos_ref, sin_ref = rest[0], rest[1]
        outs = rest[2:]
    else:
        outs = rest
    h = _norm_mod(x_ref[...], g_ref[...], sc_ref[...], sh_ref[...]).astype(BF16)
    offs = [sum(seg[:s]) for s in range(len(seg))]

    def project(s):
        return jnp.dot(h, w_ref[:, offs[s]:offs[s] + seg[s]], preferred_element_type=F32)

    nxt = project(0)
    for s, (o_ref, rp) in enumerate(zip(outs, rope)):
        r, nxt = nxt, (project(s + 1) if s + 1 < len(seg) else None)
        if rp:
            r = _rope_cols(r, cos_ref[...], sin_ref[...])
        o_ref[...] = r


def _inproj(x, gain, mod, layer, ctx, w, seg, rope=None, tables=None):
    b, t, _ = x.shape
    tm = _row_tile(t)
    rope = rope or (False,) * len(seg)
    in_specs = [
        pl.BlockSpec((None, tm, D_MODEL), lambda bb, i: (bb, i, 0)),
        pl.BlockSpec((None, 1, D_MODEL), lambda bb, i: (layer, 0, 0)),
        _mod_spec(layer, 1, ctx),
        _mod_spec(layer, 0, ctx),
        _resident(),
    ]
    args = [x, gain, mod, mod, w]
    if any(rope):
        in_specs += [pl.BlockSpec((tm, LANES), lambda bb, i: (i, 0))] * 2
        args += list(tables)
    return pl.pallas_call(
        functools.partial(_inproj_kernel, seg=tuple(seg), rope=tuple(rope)),
        grid=(b, t // tm),
        in_specs=in_specs,
        out_specs=[pl.BlockSpec((None, tm, n), lambda bb, i: (bb, i, 0)) for n in seg],
        out_shape=[jax.ShapeDtypeStruct((b, t, n), F32) for n in seg],
        compiler_params=_cparams(("parallel", "parallel")),
        name="inproj",
    )(*args)


def _inproj_dn_kernel(xp_ref, x_ref, xn_ref, g_ref, sc_ref, sh_ref, w_ref, cw_ref,
                      q_ref, k_ref, v_ref, z_ref, ab_ref, *, nt):
    i = pl.program_id(1)
    tm = x_ref.shape[0]
    hw = DN_HEADS * DN_DK
    xa = jnp.concatenate([xp_ref[...], x_ref[...], xn_ref[...]], axis=0)
    h = _norm_mod(xa, g_ref[...], sc_ref[...], sh_ref[...]).astype(BF16)
    zero8 = jnp.zeros((SUBLANES, hw), F32)
    hc = h[SUBLANES:SUBLANES + tm]
    proj = [lambda s=s: jnp.dot(h, w_ref[:, s * hw:(s + 1) * hw], preferred_element_type=F32) for s in range(3)]
    proj.append(lambda: (jnp.dot(hc, w_ref[:, 3 * hw:4 * hw], preferred_element_type=F32),
                         jnp.dot(hc, w_ref[:, 4 * hw:], preferred_element_type=F32)))
    nxt = proj[0]()
    for s, (o_ref, l2_scale) in enumerate(((q_ref, DN_DK ** -0.5), (k_ref, 1.0), (v_ref, None))):
        cols = slice(s * hw, (s + 1) * hw)
        r, nxt = nxt, proj[s + 1]()
        prev8 = jnp.where(i > 0, r[:SUBLANES], zero8)
        next8 = jnp.where(i < nt - 1, r[SUBLANES + tm:], zero8)
        y = _silu(_conv4(prev8, r[SUBLANES:SUBLANES + tm], next8, cw_ref[:, cols], 0.0))
        if l2_scale is not None:
            heads = []
            for hh in range(DN_HEADS):
                yh = y[:, hh * DN_DK:(hh + 1) * DN_DK]
                inv = lax.rsqrt(jnp.sum(yh * yh, axis=-1, keepdims=True) + EPS)
                heads.append(yh * (inv * l2_scale) if l2_scale != 1.0 else yh * inv)
            y = jnp.concatenate(heads, axis=1)
        o_ref[...] = y
    z_ref[...], ab_ref[...] = nxt


def _inproj_dn(x, gain, mod, layer, ctx, w, conv_w):
    b, t, _ = x.shape
    tm = _row_tile(t)
    nt = t // tm
    r8 = tm // SUBLANES
    last8 = t // SUBLANES - 1
    hw = DN_HEADS * DN_DK
    widths = (hw, hw, hw, DN_HEADS * DN_DV, LANES)
    return pl.pallas_call(
        functools.partial(_inproj_dn_kernel, nt=nt),
        grid=(b, nt),
        in_specs=[
            pl.BlockSpec((None, SUBLANES, D_MODEL), lambda bb, i: (bb, jnp.maximum(i * r8 - 1, 0), 0)),
            pl.BlockSpec((None, tm, D_MODEL), lambda bb, i: (bb, i, 0)),
            pl.BlockSpec((None, SUBLANES, D_MODEL), lambda bb, i: (bb, jnp.minimum((i + 1) * r8, last8), 0)),
            pl.BlockSpec((None, 1, D_MODEL), lambda bb, i: (layer, 0, 0)),
            _mod_spec(layer, 1, ctx), _mod_spec(layer, 0, ctx), _resident(),
            pl.BlockSpec((CONV_W, 3 * hw), lambda bb, i: (0, 0))],
        out_specs=[pl.BlockSpec((None, tm, n), lambda bb, i: (bb, i, 0)) for n in widths],
        out_shape=[jax.ShapeDtypeStruct((b, t, n), F32) for n in widths],
        compiler_params=_cparams(("parallel", "parallel")),
        name="inproj_dn",
    )(x, x, x, gain, mod, mod, w, conv_w)


def _softplus(z):
    return jnp.maximum(z, 0.0) + jnp.log1p(jnp.exp(-jnp.abs(z)))


def _rg_coeffs(xc, wg_ref, ba_ref, bx_ref, lam_ref, d):
    xb = xc.astype(BF16)
    half = D_RNN // 2
    r_parts, i_parts = [], []
    for hh in range(2):
        res = jnp.dot(xb[:, hh * half:(hh + 1) * half], wg_ref[d, hh], preferred_element_type=F32)
        r_parts.append(res[:, :half])
        i_parts.append(res[:, half:])
    r = jax.nn.sigmoid(jnp.concatenate(r_parts, axis=1) + ba_ref[d])
    i = jax.nn.sigmoid(jnp.concatenate(i_parts, axis=1) + bx_ref[d])
    log_a = (-RG_C) * r * _softplus(-lam_ref[d])
    a = jnp.exp(log_a)
    s2 = -jnp.tanh(log_a) * (a * a + 1.0)
    b = (s2 * lax.rsqrt(jnp.maximum(s2, jnp.finfo(jnp.float32).tiny))) * (i * xc)
    return a, b


def _conv4(prev8, cur, next8, w_ref, bias):
    tm = cur.shape[0]
    xp = jnp.concatenate([prev8, cur, next8], axis=0)
    n = tm + 2 * SUBLANES
    y = xp[SUBLANES:SUBLANES + tm] * w_ref[2:3, :]
    y = y + pltpu.roll(xp, 2, 0)[SUBLANES:SUBLANES + tm] * w_ref[0:1, :]
    y = y + pltpu.roll(xp, 1, 0)[SUBLANES:SUBLANES + tm] * w_ref[1:2, :]
    y = y + pltpu.roll(xp, n - 1, 0)[SUBLANES:SUBLANES + tm] * w_ref[3:4, :]
    return y + bias


def _scan_group(a, b, carry, reverse):
    row = lax.broadcasted_iota(jnp.int32, (SUBLANES, D_RNN), 0)
    for s in (1, 2, 4):
        if reverse:
            m = row < SUBLANES - s
            sh = SUBLANES - s
        else:
            m = row >= s
            sh = s
        a_sh = pltpu.roll(a, sh, 0)
        b_sh = pltpu.roll(b, sh, 0)
        b = jnp.where(m, a * b_sh + b, b)
        a = jnp.where(m, a * a_sh, a)
    h = a * carry + b
    return h, (h[0:1, :] if reverse else h[SUBLANES - 1:SUBLANES, :])


def _scan_both(af_ref, bf_ref, ab_ref, bb_ref, hf_ref, hb_ref, carry_f, carry_b, tm):
    ng = tm // SUBLANES

    def body(g, carry):
        cf, cb = carry
        rf = pl.multiple_of(g * SUBLANES, SUBLANES)
        rb = pl.multiple_of((ng - 1 - g) * SUBLANES, SUBLANES)
        hf, cf = _scan_group(af_ref[pl.ds(rf, SUBLANES), :], bf_ref[pl.ds(rf, SUBLANES), :], cf, False)
        hb, cb = _scan_group(ab_ref[pl.ds(rb, SUBLANES), :], bb_ref[pl.ds(rb, SUBLANES), :], cb, True)
        hf_ref[pl.ds(rf, SUBLANES), :] = hf
        hb_ref[pl.ds(rb, SUBLANES), :] = hb
        return cf, cb

    return lax.fori_loop(0, ng, body, (carry_f, carry_b))


def _rglru_kernel(xf_p, xf_c, xf_n, xb_p, xb_c, xb_n, cw_ref, cb_ref, wg_ref, ba_ref, bx_ref, lam_ref,
                  h0_ref, hf_ref, hb_ref, st_ref, af_s, bf_s, ab_s, bb_s, carry_s, *, tm, nt):
    i = pl.program_id(1)

    @pl.when(i == 0)
    def _():
        carry_s[...] = h0_ref[...]

    zero8 = jnp.zeros((SUBLANES, D_RNN), F32)
    prev8 = jnp.where(i > 0, xf_p[...], zero8)
    next8 = jnp.where(i < nt - 1, xf_n[...], zero8)
    xc = _conv4(prev8, xf_c[...], next8, cw_ref, cb_ref[...])
    af_s[...], bf_s[...] = _rg_coeffs(xc, wg_ref, ba_ref, bx_ref, lam_ref, 0)
    j = nt - 1 - i
    prev8 = jnp.where(j > 0, xb_p[...], zero8)
    next8 = jnp.where(j < nt - 1, xb_n[...], zero8)
    xc = _conv4(prev8, xb_c[...], next8, cw_ref, cb_ref[...])
    ab_s[...], bb_s[...] = _rg_coeffs(xc, wg_ref, ba_ref, bx_ref, lam_ref, 1)
    cf, cb = _scan_both(af_s, bf_s, ab_s, bb_s, hf_ref, hb_ref, carry_s[0:1, :], carry_s[1:2, :], tm)
    carry_s[0:1, :] = cf
    carry_s[1:2, :] = cb
    st_ref[...] = carry_s[...]


def _rglru(xg, conv_w, conv_b, wg, ba, bx, lam, h0):
    b, t, _ = xg.shape
    tm = _row_tile(t)
    nt = t // tm
    r8 = tm // SUBLANES
    last8 = t // SUBLANES - 1

    def cur(f):
        return pl.BlockSpec((None, tm, D_RNN), lambda bb, i: (bb, f(i), 0))

    def prv(f):
        return pl.BlockSpec((None, SUBLANES, D_RNN), lambda bb, i: (bb, jnp.maximum(f(i) * r8 - 1, 0), 0))

    def nxt(f):
        return pl.BlockSpec((None, SUBLANES, D_RNN), lambda bb, i: (bb, jnp.minimum((f(i) + 1) * r8, last8), 0))

    fw = lambda i: i
    bw = lambda i: nt - 1 - i
    full = lambda shape: pl.BlockSpec(shape, lambda bb, i: (0,) * len(shape))
    return pl.pallas_call(
        functools.partial(_rglru_kernel, tm=tm, nt=nt),
        grid=(b, nt),
        in_specs=[prv(fw), cur(fw), nxt(fw), prv(bw), cur(bw), nxt(bw),
                  full((CONV_W, D_RNN)), full((1, D_RNN)),
                  full((2, 2, D_RNN // 2, D_RNN)), full((2, 1, D_RNN)), full((2, 1, D_RNN)), full((2, 1, D_RNN)),
                  pl.BlockSpec((None, 2, D_RNN), lambda bb, i: (bb, 0, 0))],
        out_specs=[cur(fw), cur(bw), pl.BlockSpec((None, 2, D_RNN), lambda bb, i: (bb, 0, 0))],
        out_shape=[jax.ShapeDtypeStruct((b, t, D_RNN), F32), jax.ShapeDtypeStruct((b, t, D_RNN), F32),
                   jax.ShapeDtypeStruct((b, 2, D_RNN), F32)],
        scratch_shapes=[pltpu.VMEM((tm, D_RNN), F32)] * 4 + [pltpu.VMEM((2, D_RNN), F32)],
        compiler_params=_cparams(("parallel", "arbitrary")),
        name="rglru",
    )(xg, xg, xg, xg, xg, xg, conv_w, conv_b, wg, ba, bx, lam, h0)


def _attend(q, kcat, vcat, sink_ref, valid, o_ref):
    scale = HEAD_DIM ** -0.5
    kb = kcat.astype(BF16)
    vb = vcat.astype(BF16)
    qb = (q * scale).astype(BF16)
    ones = jnp.ones((kcat.shape[0], HEAD_DIM), BF16)
    v_ext = [jnp.concatenate([vb[:, kv * HEAD_DIM:(kv + 1) * HEAD_DIM], ones], axis=1) for kv in range(N_KV)]
    def scores(h):
        kv = h // Q_GROUP
        qh = qb[:, h * HEAD_DIM:(h + 1) * HEAD_DIM]
        kh = kb[:, kv * HEAD_DIM:(kv + 1) * HEAD_DIM]
        return lax.dot_general(qh, kh, (((1,), (1,)), ((), ())), preferred_element_type=F32)

    ahead = 1
    queue = [scores(h) for h in range(ahead)]
    for h in range(N_HEADS):
        kv = h // Q_GROUP
        s = queue.pop(0)
        if h + ahead < N_HEADS:
            queue.append(scores(h + ahead))
        if valid is not None:
            s = jnp.where(valid, s, NEG)
        sk = sink_ref[h:h + 1, 0:1]
        m = jnp.maximum(jnp.max(s, axis=-1, keepdims=True), sk)
        p = jnp.exp(s - m)
        ov = jnp.dot(p.astype(BF16), v_ext[kv], preferred_element_type=F32)
        den = ov[:, HEAD_DIM:HEAD_DIM + 1] + jnp.exp(sk - m)
        o_ref[:, h * HEAD_DIM:(h + 1) * HEAD_DIM] = ov[:, :HEAD_DIM] / den


def _ctx_attn_kernel(q_ref, k_ref, v_ref, sink_ref, o_ref):
    _attend(q_ref[...], k_ref[...], v_ref[...], sink_ref, None, o_ref)


def _lat_attn_kernel(q_ref, kp_ref, kc_ref, kn_ref, vp_ref, vc_ref, vn_ref, ck_ref, cv_ref, sink_ref, o_ref,
                     *, tq, t):
    i = pl.program_id(1)
    past = ck_ref.shape[0]
    kcat = jnp.concatenate([kp_ref[...], kc_ref[...], kn_ref[...], ck_ref[...]], axis=0)
    vcat = jnp.concatenate([vp_ref[...], vc_ref[...], vn_ref[...], cv_ref[...]], axis=0)
    nwin = tq + 2 * WINDOW
    col = lax.broadcasted_iota(jnp.int32, (tq, nwin + past), 1)
    rowq = lax.broadcasted_iota(jnp.int32, (tq, nwin + past), 0)
    kpos = i * tq - WINDOW + col
    qpos = i * tq + rowq
    in_win = (jnp.abs(kpos - qpos) <= WINDOW) & (kpos >= 0) & (kpos < t)
    valid = (col >= nwin) | in_win
    _attend(q_ref[...], kcat, vcat, sink_ref, valid, o_ref)


def _attention(q, k, v, sink_b, ctx_k=None, ctx_v=None, layer_i=0):
    b, t, _ = q.shape
    kvw = N_KV * HEAD_DIM
    qw = N_HEADS * HEAD_DIM
    if ctx_k is None:
        return pl.pallas_call(
            _ctx_attn_kernel,
            grid=(b,),
            in_specs=[pl.BlockSpec((None, t, qw), lambda bb: (bb, 0, 0)),
                      pl.BlockSpec((None, t, kvw), lambda bb: (bb, 0, 0)),
                      pl.BlockSpec((None, t, kvw), lambda bb: (bb, 0, 0)),
                      pl.BlockSpec((N_HEADS, LANES), lambda bb: (0, 0))],
            out_specs=pl.BlockSpec((None, t, qw), lambda bb: (bb, 0, 0)),
            out_shape=jax.ShapeDtypeStruct((b, t, qw), F32),
            compiler_params=_cparams(("parallel",)),
            name="ctx_attn",
        )(q, k, v, sink_b)
    tq = ATTN_Q_TILE
    assert t % tq == 0 and tq % WINDOW == 0, (t, tq)
    m = tq // WINDOW
    lastw = t // WINDOW - 1
    past = ctx_k.shape[2]
    cur = pl.BlockSpec((None, tq, kvw), lambda bb, i: (bb, i, 0))
    prv = pl.BlockSpec((None, WINDOW, kvw), lambda bb, i: (bb, jnp.maximum(i * m - 1, 0), 0))
    nxt = pl.BlockSpec((None, WINDOW, kvw), lambda bb, i: (bb, jnp.minimum((i + 1) * m, lastw), 0))
    cache = pl.BlockSpec((None, None, past, kvw), lambda bb, i: (bb, layer_i, 0, 0))
    return pl.pallas_call(
        functools.partial(_lat_attn_kernel, tq=tq, t=t),
        grid=(b, t // tq),
        in_specs=[pl.BlockSpec((None, tq, qw), lambda bb, i: (bb, i, 0)),
                  prv, cur, nxt, prv, cur, nxt, cache, cache,
                  pl.BlockSpec((N_HEADS, LANES), lambda bb, i: (0, 0))],
        out_specs=pl.BlockSpec((None, tq, qw), lambda bb, i: (bb, i, 0)),
        out_shape=jax.ShapeDtypeStruct((b, t, qw), F32),
        compiler_params=_cparams(("parallel", "parallel")),
        name="lat_attn",
    )(q, k, k, k, v, v, v, ctx_k, ctx_v, sink_b)


def _gelu_tanh(x):
    c = math.sqrt(2.0 / math.pi)
    return 0.5 * x * (1.0 + jnp.tanh(c * (x + 0.044715 * (x * x * x))))


def _even_mix(hf_ref, hb_ref, gr_ref, ya_ref, w_ref):
    yr = ((hf_ref[...] + hb_ref[...]) * _gelu_tanh(gr_ref[...])).astype(BF16)
    out = jnp.dot(yr, w_ref[0:D_RNN, :], preferred_element_type=F32)
    return out + jnp.dot(ya_ref[...].astype(BF16), w_ref[D_RNN:, :], preferred_element_type=F32)


def _odd_mix(o_ref, z_ref, gn_ref, w_ref):
    parts = []
    for h in range(DN_HEADS):
        sl = slice(h * DN_DV, (h + 1) * DN_DV)
        oh = o_ref[:, sl]
        ms = jnp.mean(oh * oh, axis=-1, keepdims=True)
        y = (oh * lax.rsqrt(ms + EPS) * gn_ref[...]) * _silu(z_ref[:, sl])
        parts.append(y.astype(BF16))
    return jnp.dot(jnp.concatenate(parts, axis=1), w_ref[...], preferred_element_type=F32)


FF_CHUNK = 256


def _tail_kernel(*refs, even, final):
    n_mix = 4 if even else 3
    mix_refs, rest = refs[:n_mix], refs[n_mix:]
    (x_ref, ga1_ref, wo_ref, g_ref, sc_ref, sh_ref, ga2_ref, wgu_ref, wd_ref, gf_ref, o_ref, act_s) = rest
    mix = _even_mix(*mix_refs, wo_ref) if even else _odd_mix(*mix_refs, wo_ref)
    x = x_ref[...] + ga1_ref[...] * mix
    h = _norm_mod(x, g_ref[...], sc_ref[...], sh_ref[...]).astype(BF16)
    for c in range(D_FF // FF_CHUNK):
        lo = c * FF_CHUNK
        gate = jnp.dot(h, wgu_ref[:, lo:lo + FF_CHUNK], preferred_element_type=F32)
        up = jnp.dot(h, wgu_ref[:, D_FF + lo:D_FF + lo + FF_CHUNK], preferred_element_type=F32)
        act_s[:, lo:lo + FF_CHUNK] = (_silu(gate) * up).astype(BF16)
    y = x + ga2_ref[...] * jnp.dot(act_s[...], wd_ref[...], preferred_element_type=F32)
    if final:
        ms = jnp.mean(y * y, axis=-1, keepdims=True)
        y = y * lax.rsqrt(ms + EPS) * gf_ref[...]
    o_ref[...] = y


def _tail(mix_args, mix_specs, even, x, gain, mod, layer, ctx, w_out, wgu, wd, g_final, final):
    b, t, _ = x.shape
    tm = _row_tile(t)
    row = pl.BlockSpec((None, tm, D_MODEL), lambda bb, i: (bb, i, 0))
    return pl.pallas_call(
        functools.partial(_tail_kernel, even=even, final=final),
        grid=(b, t // tm),
        in_specs=list(mix_specs(tm)) + [
            row, _mod_spec(layer, 2, ctx), _resident(),
            pl.BlockSpec((None, 1, D_MODEL), lambda bb, i: (layer, 0, 0)),
            _mod_spec(layer, 4, ctx), _mod_spec(layer, 3, ctx), _mod_spec(layer, 5, ctx),
            _resident(), _resident(),
            pl.BlockSpec((1, D_MODEL), lambda bb, i: (0, 0))],
        out_specs=row,
        out_shape=jax.ShapeDtypeStruct((b, t, D_MODEL), F32),
        scratch_shapes=[pltpu.VMEM((tm, D_FF), BF16)],
        compiler_params=_cparams(("parallel", "parallel")),
        name="tail_even" if even else "tail_odd",
    )(*mix_args, x, mod, w_out, gain, mod, mod, mod, wgu, wd, g_final)


def _even_mix_specs(tm):
    half = lambda j: pl.BlockSpec((None, tm, D_RNN), lambda bb, i: (bb, i, j))
    return [half(0), half(0), half(1), half(0)]


def _odd_mix_specs(tm):
    row = pl.BlockSpec((None, tm, D_MODEL), lambda bb, i: (bb, i, 0))
    return [row, row, pl.BlockSpec((1, DN_DV), lambda bb, i: (0, 0))]


def _gates_kernel(ab_ref, alog_ref, dtb_ref, g_ref, gt_ref, *, t):
    ab = ab_ref[...]
    lane = lax.broadcasted_iota(jnp.int32, (t, LANES), 1)
    row = lax.broadcasted_iota(jnp.int32, (t, LANES), 0) % DN_CHUNK
    g = -jnp.exp(alog_ref[...]) * _softplus(ab + dtb_ref[...])
    fwd, bwd = g, g
    s = 1
    while s < DN_CHUNK:
        fwd = fwd + jnp.where(row >= s, pltpu.roll(fwd, s, 0), 0.0)
        bwd = bwd + jnp.where(row < DN_CHUNK - s, pltpu.roll(bwd, t - s, 0), 0.0)
        s *= 2
    gc = jnp.where(lane < DN_HEADS, fwd, bwd)
    out = jnp.where(lane < 2 * DN_HEADS, gc, jax.nn.sigmoid(ab))
    g_ref[...] = out
    for blk in range(t // LANES):
        gt_ref[:, blk * LANES:(blk + 1) * LANES] = out[blk * LANES:(blk + 1) * LANES, :].T


def _gates(ab, alog_row, dtb_row):
    b, t, _ = ab.shape
    return pl.pallas_call(
        functools.partial(_gates_kernel, t=t),
        grid=(b,),
        in_specs=[pl.BlockSpec((None, t, LANES), lambda bb: (bb, 0, 0)),
                  pl.BlockSpec((1, LANES), lambda bb: (0, 0)),
                  pl.BlockSpec((1, LANES), lambda bb: (0, 0))],
        out_specs=[pl.BlockSpec((None, t, LANES), lambda bb: (bb, 0, 0)),
                   pl.BlockSpec((None, LANES, t), lambda bb: (bb, 0, 0))],
        out_shape=[jax.ShapeDtypeStruct((b, t, LANES), F32), jax.ShapeDtypeStruct((b, LANES, t), F32)],
        compiler_params=_cparams(("parallel",)),
        name="dn_gates",
    )(ab, alog_row, dtb_row)


def _l2n(x):
    return x * lax.rsqrt(jnp.sum(x * x, axis=-1, keepdims=True) + EPS)


DN_GROUP = 8


def _delta_prepare(jobs, hooks=()):
    hooks = list(hooks)

    def tick():
        if hooks:
            hooks.pop(0)()

    c = DN_CHUNK
    ri = lax.broadcasted_iota(jnp.int32, (c, c), 0)
    ci = lax.broadcasted_iota(jnp.int32, (c, c), 1)
    eye = (ri == ci).astype(F32)

    def same_block(shift):
        return (ri >> shift) == (ci >> shift)

    def mm(a, b):
        return jnp.dot(a, b, preferred_element_type=F32)

    base = 3
    top = int(math.log2(c)) - 1
    for j in jobs:
        incl = (ri <= ci) if j['rev'] else (ri >= ci)
        j['decay'] = jnp.exp(jnp.where(incl, j['gcol'] - j['grow'], NEG))
        j['kb'] = j['k'].astype(BF16)
        j['qkb'] = jnp.concatenate([j['q'], j['k']], axis=0).astype(BF16)
    for j in jobs:
        j['qkk'] = lax.dot_general(j['qkb'], j['kb'], (((1,), (1,)), ((), ())), preferred_element_type=F32)
    tick()
    for j in jobs:
        strict = (ri < ci) if j['rev'] else (ri > ci)
        j['qk'] = (j['qkk'][:c] * j['decay']).astype(BF16)
        j['n'] = jnp.where(strict, j['beta'] * j['qkk'][c:] * j['decay'], 0.0)
        dg = jnp.where(same_block(base), j['n'], 0.0)
        j['tinv'] = eye - dg
        j['dgb'] = dg.astype(BF16)
    for j in jobs:
        j['m'] = mm(j['dgb'], j['dgb'])
    tick()
    for level in range(1, base):
        last = level == base - 1
        for j in jobs:
            rhs = j['tinv'] if last else jnp.concatenate([j['tinv'], j['m']], axis=1)
            j['both'] = mm(j['m'].astype(BF16), rhs.astype(BF16))
        tick()
        for j in jobs:
            j['tinv'] = j['tinv'] + j['both'][:, :c]
            if not last:
                j['m'] = j['both'][:, c:]
    def as_bf16_mask(m):
        return m.astype(F32).astype(BF16)

    for j in jobs:
        j['tb'] = j['tinv'].astype(BF16)
        j['nb'] = j['n'].astype(BF16)
    for shift in range(base, top):
        keep = as_bf16_mask(same_block(shift + 1) & jnp.logical_not(same_block(shift)))
        for j in jobs:
            j['inner'] = mm(j['nb'] * keep, j['tb'])
        tick()
        for j in jobs:
            j['upd'] = mm(j['tb'], j['inner'].astype(BF16))
        tick()
        for j in jobs:
            j['tb'] = j['tb'] - j['upd'].astype(BF16)
    keep = as_bf16_mask(jnp.logical_not(same_block(top)))
    for j in jobs:
        j['off'] = j['nb'] * keep
        j['eg'] = jnp.exp(j['gcol'])
        j['rhs'] = jnp.concatenate([j['v'] * j['beta'], j['k'] * (j['beta'] * j['eg'])], axis=1).astype(BF16)
    for j in jobs:
        j['y'] = mm(j['tb'], j['rhs'])
    tick()
    for j in jobs:
        j['inner'] = mm(j['off'], j['y'].astype(BF16))
    tick()
    for j in jobs:
        j['upd'] = mm(j['tb'], j['inner'].astype(BF16))
    tick()
    for j in jobs:
        glast = j['gcol'][0:1, :] if j['rev'] else j['gcol'][c - 1:c, :]
        j['sol'] = (j['y'] - j['upd']).astype(BF16)
        j['kd'] = (j['k'] * jnp.exp(glast - j['gcol'])).astype(BF16)
        j['gl'] = jnp.broadcast_to(jnp.exp(glast), (1, DN_DV))
    for j in jobs:
        j['kdsol'] = lax.dot_general(j['kd'], j['sol'], (((0,), (0,)), ((), ())), preferred_element_type=F32)
        j['qksol'] = mm(j['qk'], j['sol'])
    tick()
    for j in jobs:
        j['b_mat'] = j['kdsol'][:, :DN_DV]
        j['m_mat'] = j['kdsol'][:, DN_DV:].astype(BF16)
        j['r_mat'] = j['qksol'][:, :DN_DV]
        j['q_mat'] = (j['q'] * j['eg'] - j['qksol'][:, DN_DV:]).astype(BF16)
    while hooks:
        tick()
    return jobs


def _delta_kernel(q_ref, k_ref, v_ref, g_ref, gt_ref, s0_ref, o_ref, so_ref,
                  m_s, b_s, q_s, r_s, gl_s, *, t, group, hb):
    h0 = pl.program_id(1) * hb
    nc = t // DN_CHUNK
    ntile = nc // group
    so_ref[...] = s0_ref[...]
    o_ref[...] = jnp.zeros_like(o_ref)
    lane = lax.broadcasted_iota(jnp.int32, (DN_CHUNK, LANES), 1)
    head_row = lax.broadcasted_iota(jnp.int32, (DN_HEADS, LANES), 0)

    def pick(tile, idx):
        return jnp.sum(jnp.where(lane == idx, tile, 0.0), axis=1, keepdims=True)

    def tile_base(it, d):
        tile = it if d == 0 else ntile - 1 - it
        start = tile * (group * DN_CHUNK)
        return start if isinstance(start, int) else pl.multiple_of(start, LANES)

    def prepare(it, hooks=()):
        slot = it % 2
        jobs = []
        for hh in range(hb):
            h = h0 + hh
            cols = slice(hh * LANES, (hh + 1) * LANES)
            for d in range(2):
                base = tile_base(it, d)
                for gi in range(group):
                    rows = pl.ds(base + gi * DN_CHUNK, DN_CHUNK)
                    gtile = g_ref[rows, :]
                    gheads = gt_ref[d * DN_HEADS:(d + 1) * DN_HEADS, pl.ds(base + (gi // 2) * LANES, LANES)]
                    grow2 = jnp.sum(jnp.where(head_row == h, gheads, 0.0), axis=0, keepdims=True)
                    jobs.append(dict(
                        hh=hh, d=d, gi=gi, rev=(d == 1),
                        q=q_ref[rows, cols], k=k_ref[rows, cols], v=v_ref[rows, cols],
                        gcol=pick(gtile, d * DN_HEADS + h),
                        beta=pick(gtile, 2 * DN_HEADS + d * DN_HEADS + h),
                        grow=grow2[:, (gi % 2) * DN_CHUNK:(gi % 2 + 1) * DN_CHUNK]))
        _delta_prepare(jobs, hooks)
        for j in jobs:
            hh, d, gi = j['hh'], j['d'], j['gi']
            m_s[slot, hh, d, gi] = j['m_mat']
            b_s[slot, hh, d, gi] = j['b_mat']
            q_s[slot, hh, d, gi] = j['q_mat']
            r_s[slot, hh, d, gi] = j['r_mat']
            gl_s[slot, hh, d, gi] = j['gl']

    def apply_steps(it):
        slot = it % 2

        def make(step):
            def run():
                for hh in range(hb):
                    cols = slice(hh * LANES, (hh + 1) * LANES)
                    for d in range(2):
                        gi = step if d == 0 else group - 1 - step
                        rows = pl.ds(tile_base(it, d) + gi * DN_CHUNK, DN_CHUNK)
                        s = so_ref[d, hh]
                        sb = s.astype(BF16)
                        o_ref[rows, cols] += (jnp.dot(q_s[slot, hh, d, gi], sb, preferred_element_type=F32)
                                              + r_s[slot, hh, d, gi])
                        so_ref[d, hh] = (gl_s[slot, hh, d, gi] * s
                                         - jnp.dot(m_s[slot, hh, d, gi], sb, preferred_element_type=F32)
                                         + b_s[slot, hh, d, gi])
            return run

        return [make(step) for step in range(group)]

    prepare(0)

    def body(it, carry):
        prepare(it, apply_steps(it - 1))
        return carry

    lax.fori_loop(1, ntile, body, 0)
    for run in apply_steps(ntile - 1):
        run()


def _delta(q, k, v, g, gt, s0):
    b, t, _ = q.shape
    hd = DN_HEADS
    hb = 2
    grp = min(DN_GROUP // hb, t // DN_CHUNK)
    assert hd % hb == 0 and t % (grp * DN_CHUNK) == 0 and (grp * DN_CHUNK) % LANES == 0, (t, grp, hb)
    seq = pl.BlockSpec((None, t, hb * LANES), lambda bb, h: (bb, 0, h))
    st = pl.BlockSpec((None, 2, hb, DN_DK, DN_DV), lambda bb, h: (bb, 0, h, 0, 0))
    return pl.pallas_call(
        functools.partial(_delta_kernel, t=t, group=grp, hb=hb),
        grid=(b, hd // hb),
        in_specs=[seq, seq, seq,
                  pl.BlockSpec((None, t, LANES), lambda bb, h: (bb, 0, 0)),
                  pl.BlockSpec((None, LANES, t), lambda bb, h: (bb, 0, 0)),
                  st],
        out_specs=[seq, st],
        out_shape=[jax.ShapeDtypeStruct((b, t, hd * DN_DV), F32),
                   jax.ShapeDtypeStruct((b, 2, hd, DN_DK, DN_DV), F32)],
        scratch_shapes=[
            pltpu.VMEM((2, hb, 2, grp, DN_DK, DN_DK), BF16), pltpu.VMEM((2, hb, 2, grp, DN_DK, DN_DV), F32),
            pltpu.VMEM((2, hb, 2, grp, DN_CHUNK, DN_DK), BF16), pltpu.VMEM((2, hb, 2, grp, DN_CHUNK, DN_DV), F32),
            pltpu.VMEM((2, hb, 2, grp, 1, DN_DV), F32)],
        compiler_params=_cparams(("parallel", "parallel")),
        name="delta",
    )(q, k, v, g, gt, s0)


def _block_diag_gates(w_a, w_x):
    half_blocks = RG_BLOCKS // 2

    def bd(w):
        eye = jnp.eye(half_blocks, dtype=w.dtype)
        return jnp.einsum('ncd,nm->ncmd', w, eye).reshape(half_blocks * RG_BLOCK, half_blocks * RG_BLOCK)

    dirs = []
    for d in range(2):
        halves = []
        for hh in range(2):
            sl = slice(hh * half_blocks, (hh + 1) * half_blocks)
            halves.append(jnp.concatenate([bd(w_a[d, sl]), bd(w_x[d, sl])], axis=1))
        dirs.append(jnp.stack(halves))
    return jnp.stack(dirs).astype(BF16)


def _rope_tables(t):
    rows = t // GRID_W
    row = jnp.repeat(jnp.arange(rows, dtype=F32), GRID_W)
    col = jnp.tile(jnp.arange(GRID_W, dtype=F32), rows)
    inv = ROPE_BASE ** (-jnp.arange(0, ROPE_AXIS, 2, dtype=F32) / ROPE_AXIS)
    ar, ac = row[:, None] * inv, col[:, None] * inv
    cos = jnp.concatenate([jnp.cos(ar), jnp.cos(ar), jnp.cos(ac), jnp.cos(ac)], axis=1)
    sin = jnp.concatenate([-jnp.sin(ar), jnp.sin(ar), -jnp.sin(ac), jnp.sin(ac)], axis=1)
    reps = LANES // HEAD_DIM
    return jnp.tile(cos, (1, reps)), jnp.tile(sin, (1, reps))


def _pad_lanes(v, width=LANES):
    v = v.reshape(1, -1)
    return jnp.pad(v, ((0, 0), (0, width - v.shape[1])))


def _trunk(x, mod, ctx, p, caches):
    b, t, _ = x.shape
    ks, vs, rgs, dns = [], [], [], []
    tables = None if ctx else _rope_tables(t)
    for l in range(DEPTH):
        i = l // 2
        if l % 2 == 0:
            seg = (2 * D_RNN, N_HEADS * HEAD_DIM, N_KV * HEAD_DIM, N_KV * HEAD_DIM)
            rope = None if ctx else (False, True, True, False)
            xg, q, k, v = _inproj(x, p['g_mix'], mod, l, ctx, p['w_in_even'][i], seg, rope, tables)
            if ctx:
                h0 = jnp.zeros((b, 2, D_RNN), F32)
                ck = cv = None
            else:
                h0 = caches[2][:, i]
                ck, cv = caches[0], caches[1]
            hf, hb, st = _rglru(xg, p['conv_rg'][i], p['conv_rg_b'][i], p['rg_wg'][i], p['rg_ba'][i],
                                p['rg_bx'][i], p['rg_lam'][i], h0)
            ya = _attention(q, k, v, p['sink'][i], ck, cv, i)
            mix, specs, w_out = (hf, hb, xg, ya), _even_mix_specs, p['w_out_even'][i]
            ks.append(k)
            vs.append(v)
            rgs.append(st)
        else:
            qn, kn, vn, z, ab = _inproj_dn(x, p['g_mix'], mod, l, ctx, p['w_in_odd'][i], p['conv_dn'][i])
            g, gt = _gates(ab, p['dn_alog'][i], p['dn_dtb'][i])
            s0 = jnp.zeros((b, 2, DN_HEADS, DN_DK, DN_DV), F32) if ctx else caches[3][:, i]
            o, st = _delta(qn, kn, vn, g, gt, s0)
            mix, specs, w_out = (o, z, p['dn_o_norm'][i]), _odd_mix_specs, p['w_out_odd'][i]
            dns.append(st)
        x = _tail(mix, specs, l % 2 == 0, x, p['g_ffn'], mod, l, ctx, w_out, p['w_gu'][l], p['w_down'][l],
                  p['g_final'], l == DEPTH - 1)
    return x, ks, vs, rgs, dns


def kernel(x_prompt, x_sample, c, cache_attn_k, cache_attn_v, state_rglru, state_delta, c_ctx, w_mod, b_mod,
           g_mix, g_ffn, g_final, w_in_even, conv_rg, conv_rg_b, rg_wa, rg_ba, rg_wx, rg_bx, rg_lam, attn_sink,
           w_out_even, w_in_odd, conv_dn, dn_a_log, dn_dt_bias, dn_o_norm, w_out_odd, w_gu, w_down):
    n_even, n_odd = w_in_even.shape[0], w_in_odd.shape[0]
    nb = c.shape[0]
    assert nb <= N_MOD_ROWS // 2, nb
    assert x_sample.shape[1] % GRID_W == 0 and x_prompt.shape[1] % LANES == 0 and x_sample.shape[1] % LANES == 0
    cond = jnp.zeros((N_MOD_ROWS, D_MODEL), F32).at[:nb].set(c).at[N_MOD_ROWS // 2].set(c_ctx)
    mod = _modulation(cond, w_mod, b_mod)

    odd_in = w_in_odd.shape[2]
    odd_pad = DN_QKV + DN_HEADS * DN_DV + LANES - odd_in
    p = {
        'g_mix': g_mix.reshape(DEPTH, 1, D_MODEL), 'g_ffn': g_ffn.reshape(DEPTH, 1, D_MODEL), 'g_final': g_final.reshape(1, D_MODEL),
        'w_in_even': w_in_even.astype(BF16),
        'conv_rg': conv_rg, 'conv_rg_b': conv_rg_b.reshape(n_even, 1, D_RNN),
        'rg_wg': jnp.stack([_block_diag_gates(rg_wa[i], rg_wx[i]) for i in range(n_even)]),
        'rg_ba': rg_ba.reshape(n_even, 2, 1, D_RNN), 'rg_bx': rg_bx.reshape(n_even, 2, 1, D_RNN),
        'rg_lam': rg_lam.reshape(n_even, 2, 1, D_RNN),
        'sink': jnp.broadcast_to(attn_sink[:, :, None], (n_even, N_HEADS, LANES)),
        'w_out_even': w_out_even.astype(BF16),
        'w_in_odd': jnp.pad(w_in_odd, ((0, 0), (0, 0), (0, odd_pad))).astype(BF16),
        'conv_dn': conv_dn,
        'dn_alog': jnp.stack([_pad_lanes(dn_a_log[i]) for i in range(n_odd)]),
        'dn_dtb': jnp.stack([_pad_lanes(dn_dt_bias[i]) for i in range(n_odd)]),
        'dn_o_norm': dn_o_norm.reshape(n_odd, 1, DN_DV),
        'w_out_odd': w_out_odd.astype(BF16),
        'w_gu': w_gu.astype(BF16), 'w_down': w_down.astype(BF16),
    }
    past = cache_attn_k.shape[2]
    caches = (cache_attn_k.reshape(nb, n_even, past, N_KV * HEAD_DIM),
              cache_attn_v.reshape(nb, n_even, past, N_KV * HEAD_DIM), state_rglru, state_delta)

    y_prompt, ks, vs, rgs, dns = _trunk(x_prompt, mod, True, p, None)
    y_sample = _trunk(x_sample, mod, False, p, caches)[0]
    bp, tp = x_prompt.shape[0], x_prompt.shape[1]
    new_k = jnp.stack(ks, axis=1).reshape(bp, n_even, tp, N_KV, HEAD_DIM)
    new_v = jnp.stack(vs, axis=1).reshape(bp, n_even, tp, N_KV, HEAD_DIM)
    return (y_prompt, y_sample, new_k, new_v, jnp.stack(rgs, axis=1), jnp.stack(dns, axis=1))
```
